```python
import jax
import jax.numpy as jnp
from jax import lax
import numpy as np

D_MODEL = 1024
BATCH = 32
SEQ = 256
DEPTH = 2
DEC_BATCH = 4
DEC_SEQ = 2048
PAST_LEN = 512

GRID_W = 64
D_MIX = D_MODEL
GROUP_W = D_MIX // 4
EPS = 1e-6

GLA_HEADS = 4
GLA_DK = GROUP_W // (2 * GLA_HEADS)
GLA_DV = GROUP_W // GLA_HEADS
GLA_RANK = 16
GLA_GATE_NORM = 16.0
GLA_CHUNK = 64

RW_HEADS = 4
RW_HD = GROUP_W // RW_HEADS
RW_W_RANK = 32
RW_A_RANK = 32
RW_G_RANK = 64
RW_LN_EPS = 64e-5

SSD_HEADS = 4
SSD_HEADDIM = GROUP_W // SSD_HEADS
SSD_GROUPS = 2
SSD_N = 64
SSD_CONV = 3
SSD_CHUNK = 64
SSD_CONV_DIM = GROUP_W + 2 * SSD_GROUPS * SSD_N

NAT_HEADS = 4
NAT_HD = GROUP_W // NAT_HEADS
NAT_KR = 8
NAT_KC = 16
Q_BLOCK = 128

N_EXPERTS = 32
TOP_K = 4
D_FF = D_MODEL
SWIGLU_LIMIT = 7.0
SWIGLU_ALPHA = 1.702
MOE_BLOCK = 128

GLA_SIZES = (GLA_HEADS * GLA_DK, GLA_HEADS * GLA_DK, GROUP_W, GROUP_W, GLA_RANK)
RW_SIZES = (GROUP_W, GROUP_W, GROUP_W, RW_W_RANK, RW_A_RANK, RW_G_RANK)
SSD_SIZES = (GROUP_W, SSD_CONV_DIM, 2 * SSD_HEADS)
NAT_SIZES = (GROUP_W, GROUP_W, GROUP_W)
RW_COLS = sum(RW_SIZES)
MIXER_COLS = (sum(GLA_SIZES), RW_COLS, sum(SSD_SIZES), sum(NAT_SIZES))
IN_COLS = sum(MIXER_COLS)

kernel_name = 'hybrid_dit_gla_rwkv7_ssd_natten_moe_step'


def _split(t, sizes):
    idx, acc = [], 0
    for s in sizes[:-1]:
        acc += s
        idx.append(acc)
    return jnp.split(t, idx, axis=-1)


def rmsnorm(x, g):
    xf = x.astype(jnp.float32)
    return (xf * lax.rsqrt(jnp.mean(xf * xf, axis=-1, keepdims=True) + EPS) * g).astype(x.dtype)


def to_heads(t, nh):
    b, l, _ = t.shape
    return t.reshape(b, l, nh, -1).transpose(0, 2, 1, 3)


def from_heads(t):
    b, h, l, d = t.shape
    return t.transpose(0, 2, 1, 3).reshape(b, l, h * d)


def flip_seq(t):
    return jnp.flip(t, axis=2)


def chunk_gla(q, k, v, log_a, s0):
    bsz, nh, seqlen, _ = q.shape
    dv = v.shape[-1]
    n = seqlen // GLA_CHUNK
    cs = lambda t: t.reshape(bsz, nh, n, GLA_CHUNK, t.shape[-1])
    q, k, v, log_a = cs(q), cs(k), cs(v), cs(log_a)
    b = jnp.cumsum(log_a, axis=3)
    b_end = b[:, :, :, -1:, :]
    q_in = q * jnp.exp(b)
    k_in = k * jnp.exp(-b)
    k_end = k * jnp.exp(b_end - b)
    causal = jnp.tril(jnp.ones((GLA_CHUNK, GLA_CHUNK), bool))
    att = jnp.where(causal, jnp.einsum('bhncd,bhnsd->bhncs', q_in, k_in), 0.0)
    o = jnp.einsum('bhncs,bhnse->bhnce', att, v)
    ds = jnp.einsum('bhnsd,bhnse->bhnde', k_end, v)
    decay = jnp.exp(b_end[:, :, :, 0, :])

    def step(s, inp):
        d, dsn = inp
        return s * d[..., None] + dsn, s

    s_fin, s_prev = lax.scan(step, s0.astype(jnp.float32),
                             (jnp.moveaxis(decay, 2, 0), jnp.moveaxis(ds, 2, 0)))
    o = o + jnp.einsum('bhncd,nbhde->bhnce', q_in, s_prev)
    return o.reshape(bsz, nh, seqlen, dv), s_fin


def head_rmsnorm(o, g):
    return o * lax.rsqrt(jnp.mean(o * o, axis=-1, keepdims=True) + EPS) * g


def gla_mixer(u, lp, s0):
    q, k, v, g, code = _split(u, GLA_SIZES)
    q = to_heads(q, GLA_HEADS) * GLA_DK ** -0.5
    k = to_heads(k, GLA_HEADS)
    v = to_heads(v, GLA_HEADS)
    la = [to_heads(jax.nn.log_sigmoid(code @ lp['gla_wa2'][d] + lp['gla_ba'][d]) / GLA_GATE_NORM, GLA_HEADS)
          for d in range(2)]
    o_f, s_f = chunk_gla(q, k, v, la[0], s0[:, 0])
    o_b, s_b = chunk_gla(flip_seq(q), flip_seq(k), flip_seq(v), flip_seq(la[1]), s0[:, 1])
    o = head_rmsnorm(o_f + flip_seq(o_b), lp['gla_norm_g'])
    return from_heads(o) * jax.nn.silu(g), jnp.stack([s_f, s_b], axis=1)


def rwkv_scan(r, w, k, v, kk, a, s0, reverse):
    def step(s, inp):
        r_t, w_t, k_t, v_t, kk_t, a_t = inp
        sa = -jnp.einsum('bhij,bhj->bhi', s, kk_t)
        s = (s * w_t[:, :, None, :] + sa[..., None] * (kk_t * a_t)[:, :, None, :]
             + v_t[..., None] * k_t[:, :, None, :])
        return s, jnp.einsum('bhij,bhj->bhi', s, r_t)
    return lax.scan(step, s0.astype(jnp.float32), (r, w, k, v, kk, a), reverse=reverse)


def rwkv_mixer(u, lp, s0):
    bsz, seqlen, _ = u.shape
    zero = jnp.zeros_like(u[:, :1])
    prev = jnp.concatenate([zero, u[:, :-1]], axis=1)
    nxt = jnp.concatenate([u[:, 1:], zero], axis=1)
    u = u + lp['rw_mu'][0] * (prev - u) + lp['rw_mu'][1] * (nxt - u)
    r, k, v, wc, ac, gc = _split(u, RW_SIZES)
    gate = jax.nn.sigmoid(gc) @ lp['rw_g2']
    a = jax.nn.sigmoid(lp['rw_a0'] + ac @ lp['rw_a2'])
    hs = lambda t: t.reshape(bsz, seqlen, RW_HEADS, RW_HD)
    kk = hs(k * lp['rw_kk'])
    kk = kk * lax.rsqrt(jnp.sum(kk * kk, axis=-1, keepdims=True) + 1e-12)
    k = k * (1.0 + (a - 1.0) * lp['rw_ka'])
    tm = lambda t: hs(t).transpose(1, 0, 2, 3)
    decays = [jnp.exp(-jnp.exp(-jax.nn.softplus(-(lp['rw_w0'][d] + jnp.tanh(wc) @ lp['rw_w2'][d])) - 0.5))
              for d in range(2)]
    kk_t = kk.transpose(1, 0, 2, 3)
    s_f, y_f = rwkv_scan(tm(r), tm(decays[0]), tm(k), tm(v), kk_t, tm(a), s0[:, 0], False)
    s_b, y_b = rwkv_scan(tm(r), tm(decays[1]), tm(k), tm(v), kk_t, tm(a), s0[:, 1], True)
    y = (y_f + y_b).transpose(1, 0, 2, 3)
    mu = jnp.mean(y, axis=-1, keepdims=True)
    var = jnp.mean(jnp.square(y - mu), axis=-1, keepdims=True)
    y = ((y - mu) * lax.rsqrt(var + RW_LN_EPS)).reshape(bsz, seqlen, GROUP_W) * lp['rw_ln_g'] + lp['rw_ln_b']
    bonus = jnp.sum(hs(r) * hs(k) * lp['rw_rk'], axis=-1, keepdims=True) * hs(v)
    y = y + bonus.reshape(bsz, seqlen, GROUP_W)
    return y * gate, jnp.stack([s_f, s_b], axis=1)


def chunk_ssd(x, dt, a_neg, bm, cm, s0):
    bsz, nh, seqlen, hp = x.shape
    n = seqlen // SSD_CHUNK
    cs = lambda t: t.reshape((bsz, nh, n, SSD_CHUNK) + t.shape[3:])
    x, dt, bm, cm = cs(x), cs(dt), cs(bm), cs(cm)
    acum = jnp.cumsum(dt * a_neg[None, :, None, None], axis=-1)
    causal = jnp.tril(jnp.ones((SSD_CHUNK, SSD_CHUNK), bool))
    lmat = jnp.exp(jnp.where(causal, acum[..., :, None] - acum[..., None, :], -jnp.inf))
    xdt = x * dt[..., None]
    scores = jnp.einsum('bhncm,bhnsm->bhncs', cm, bm) * lmat
    y = jnp.einsum('bhncs,bhnsp->bhncp', scores, xdt)
    ds = jnp.einsum('bhnsm,bhnsp->bhnmp', bm * jnp.exp(acum[..., -1:] - acum)[..., None], xdt)
    decay = jnp.exp(acum[..., -1])

    def step(s, inp):
        d, dsn = inp
        return s * d[..., None, None] + dsn, s

    s_fin, s_prev = lax.scan(step, s0.astype(jnp.float32),
                             (jnp.moveaxis(decay, 2, 0), jnp.moveaxis(ds, 2, 0)))
    y = y + jnp.einsum('bhncm,nbhmp->bhncp', cm * jnp.exp(acum)[..., None], s_prev)
    return y.reshape(bsz, nh, seqlen, hp), s_fin


def centred_dwconv(x, w, b):
    pad = (SSD_CONV - 1) // 2
    out = lax.conv_general_dilated(x, w[:, None, :].astype(x.dtype), window_strides=(1,),
                                   padding=[(pad, SSD_CONV - 1 - pad)],
                                   dimension_numbers=('NWC', 'WIO', 'NWC'),
                                   feature_group_count=x.shape[-1])
    return out + b


def ssd_mixer(u, lp, s0):
    bsz, seqlen, _ = u.shape
    z, xbc, dt_raw = _split(u, SSD_SIZES)
    xbc = jax.nn.silu(centred_dwconv(xbc, lp['ssd_conv_w'], lp['ssd_conv_b']))
    xs, bm, cm = _split(xbc, (GROUP_W, SSD_GROUPS * SSD_N, SSD_GROUPS * SSD_N))
    x_h = to_heads(xs, SSD_HEADS)
    rep = SSD_HEADS // SSD_GROUPS
    bm = jnp.repeat(to_heads(bm, SSD_GROUPS), rep, axis=1)
    cm = jnp.repeat(to_heads(cm, SSD_GROUPS), rep, axis=1)
    dt_raw = dt_raw.reshape(bsz, seqlen, 2, SSD_HEADS)
    dts = [jax.nn.softplus(dt_raw[:, :, d] + lp['ssd_dt_bias'][d]).transpose(0, 2, 1) for d in range(2)]
    a_neg = -jnp.exp(lp['ssd_a_log'])
    y_f, s_f = chunk_ssd(x_h, dts[0], a_neg[0], bm, cm, s0[:, 0])
    y_b, s_b = chunk_ssd(flip_seq(x_h), flip_seq(dts[1]), a_neg[1], flip_seq(bm), flip_seq(cm), s0[:, 1])
    y = y_f + flip_seq(y_b) + lp['ssd_d'][None, :, None, None] * x_h
    y = from_heads(y) * jax.nn.silu(z)
    return rmsnorm(y, lp['ssd_norm_g']), jnp.stack([s_f, s_b], axis=1)


def nat_context(u):
    q, k, v = _split(u, NAT_SIZES)
    q = to_heads(q, NAT_HEADS) * NAT_HD ** -0.5
    k = to_heads(k, NAT_HEADS)
    v = to_heads(v, NAT_HEADS)
    bsz, nh, seqlen, hd = q.shape
    nb = seqlen // Q_BLOCK
    qb = jnp.moveaxis(q.reshape(bsz, nh, nb, Q_BLOCK, hd), 2, 0)

    def attend(qq):
        p = jax.nn.softmax(jnp.einsum('bhqd,bhkd->bhqk', qq, k), axis=-1)
        return jnp.einsum('bhqk,bhkd->bhqd', p, v)

    o = jnp.moveaxis(lax.map(attend, qb), 0, 2).reshape(bsz, nh, seqlen, hd)
    return from_heads(o), k, v


def nat_latent(u, rpb, k_ctx, v_ctx):
    bsz, seqlen, _ = u.shape
    rows = seqlen // GRID_W
    kr = min(NAT_KR, rows)
    q, k, v = _split(u, NAT_SIZES)
    grid = lambda t: to_heads(t, NAT_HEADS).reshape(bsz, NAT_HEADS, rows, GRID_W, NAT_HD)
    q, k, v = grid(q) * NAT_HD ** -0.5, grid(k), grid(v)
    r = jnp.arange(rows)
    col = jnp.arange(GRID_W)
    row_idx = jnp.clip(r - kr // 2, 0, rows - kr)[:, None] + jnp.arange(kr)[None, :]
    c0 = jnp.clip(col - NAT_KC // 2, 0, GRID_W - NAT_KC)
    col_ok = (col[None, :] >= c0[:, None]) & (col[None, :] < c0[:, None] + NAT_KC)
    k_band = k[:, :, row_idx]
    v_band = v[:, :, row_idx]
    s_loc = jnp.einsum('bhrqd,bhrjkd->bhrqjk', q, k_band)
    rel_r = row_idx - r[:, None] + NAT_KR - 1
    rel_c = jnp.clip(col[None, :] - col[:, None] + NAT_KC - 1, 0, 2 * NAT_KC - 2)
    bias = rpb[:, rel_r][:, :, :, rel_c].transpose(0, 1, 3, 2, 4)
    s_loc = jnp.where(col_ok[:, None, :], s_loc + bias, -jnp.inf)
    s_ctx = jnp.einsum('bhrqd,bhcd->bhrqc', q, k_ctx)
    n_loc = kr * GRID_W
    s = jnp.concatenate([s_loc.reshape(bsz, NAT_HEADS, rows, GRID_W, n_loc), s_ctx], axis=-1)
    p = jax.nn.softmax(s.astype(jnp.float32), axis=-1)
    p_loc = p[..., :n_loc].reshape(s_loc.shape)
    p_ctx = p[..., n_loc:]
    o = (jnp.einsum('bhrqjk,bhrjkd->bhrqd', p_loc, v_band)
         + jnp.einsum('bhrqc,bhcd->bhrqd', p_ctx, v_ctx))
    return from_heads(o.reshape(bsz, NAT_HEADS, seqlen, NAT_HD))


def moe_ffn(h, wr, br, w1, b1, w2, b2):
    n_tok, d = h.shape
    logits = (h @ wr + br).astype(jnp.float32)
    top_v, top_i = lax.top_k(logits, TOP_K)
    gates = jax.nn.softmax(top_v, axis=-1)
    n_asg = n_tok * TOP_K
    flat_e = top_i.reshape(-1)
    order = jnp.argsort(flat_e)
    sorted_e = flat_e[order]
    tok = order // TOP_K
    counts = jnp.bincount(flat_e, length=N_EXPERTS)
    padded = (counts + MOE_BLOCK - 1) // MOE_BLOCK * MOE_BLOCK
    start = jnp.cumsum(counts) - counts
    cum_p = jnp.cumsum(padded)
    pstart = cum_p - padded
    dest = pstart[sorted_e] + jnp.arange(n_asg) - start[sorted_e]
    n_slots = (n_asg + MOE_BLOCK - 1) // MOE_BLOCK * MOE_BLOCK + N_EXPERTS * MOE_BLOCK
    n_blocks = n_slots // MOE_BLOCK
    xs = jnp.zeros((n_slots, d), h.dtype).at[dest].set(h[tok])
    block_e = jnp.minimum(jnp.sum(cum_p[None, :] <= (jnp.arange(n_blocks) * MOE_BLOCK)[:, None], axis=1),
                          N_EXPERTS - 1)

    def expert_block(args):
        xb, e = args
        gu = xb @ w1[e] + b1[e]
        g, up = gu[:, :D_FF], gu[:, D_FF:]
        g = jnp.minimum(g, SWIGLU_LIMIT)
        up = jnp.clip(up, -SWIGLU_LIMIT, SWIGLU_LIMIT)
        return ((up + 1.0) * g * jax.nn.sigmoid(SWIGLU_ALPHA * g)) @ w2[e] + b2[e]

    ys = lax.map(expert_block, (xs.reshape(n_blocks, MOE_BLOCK, d), block_e)).reshape(n_slots, d)
    contrib = ys[dest] * gates.reshape(-1)[order][:, None].astype(ys.dtype)
    return jax.ops.segment_sum(contrib, tok, num_segments=n_tok)


def trunk_layer(x, cond, lp, ctx):
    mod = (jax.nn.silu(cond) @ lp['w_mod'] + lp['b_mod'])[:, None, :]
    sh1, sc1, g1, sh2, sc2, g2 = jnp.split(mod, 6, axis=-1)
    h = rmsnorm(x, lp['norm1_g']) * (1.0 + sc1) + sh1
    u = (h @ lp['w_in']).astype(jnp.float32)
    ua, ub, uc, ud = _split(u, MIXER_COLS)
    bsz = x.shape[0]
    if ctx is None:
        s_gla = jnp.zeros((bsz, 2, GLA_HEADS, GLA_DK, GLA_DV), jnp.float32)
        s_rw = jnp.zeros((bsz, 2, RW_HEADS, RW_HD, RW_HD), jnp.float32)
        s_ssd = jnp.zeros((bsz, 2, SSD_HEADS, SSD_N, SSD_HEADDIM), jnp.float32)
    else:
        s_gla, s_rw, s_ssd, k_ctx, v_ctx = ctx
    o_a, s_gla = gla_mixer(ua, lp, s_gla)
    o_b, s_rw = rwkv_mixer(ub, lp, s_rw)
    o_c, s_ssd = ssd_mixer(uc, lp, s_ssd)
    if ctx is None:
        o_d, k_ctx, v_ctx = nat_context(ud)
    else:
        o_d = nat_latent(ud, lp['nat_rpb'], k_ctx, v_ctx)
    mix = jnp.concatenate([o_a, o_b, o_c, o_d], axis=-1).astype(x.dtype) @ lp['w_out']
    x = x + g1 * mix
    h2 = rmsnorm(x, lp['norm2_g']) * (1.0 + sc2) + sh2
    ff = moe_ffn(h2.reshape(-1, D_MODEL), lp['moe_wr'], lp['moe_br'], lp['moe_w1'], lp['moe_b1'],
                 lp['moe_w2'], lp['moe_b2']).reshape(x.shape)
    x = x + g2 * ff
    return x, (s_gla, s_rw, s_ssd, k_ctx, v_ctx)


def setup_inputs(seed: int = 0) -> dict:
    key = jax.random.key(seed)
    ks = list(jax.random.split(key, 64))
    d, L = D_MODEL, DEPTH

    def nrm(shape, scale=1.0):
        return scale * jax.random.normal(ks.pop(), shape, jnp.float32)

    def gain(shape, noise=0.02):
        return 1.0 + noise * jax.random.normal(ks.pop(), shape, jnp.float32)

    def unif(shape, lo, hi):
        return jax.random.uniform(ks.pop(), shape, jnp.float32, lo, hi)

    dt0 = jnp.exp(unif((L, 2, SSD_HEADS), float(np.log(1e-3)), float(np.log(1e-1))))
    return {
        'x_prompt': nrm((BATCH, SEQ, d)),
        'x_sample': nrm((DEC_BATCH, DEC_SEQ, d)),
        'c': nrm((DEC_BATCH, d)),
        'c_ctx': nrm((d,)),
        'state_gla': nrm((DEC_BATCH, L, 2, GLA_HEADS, GLA_DK, GLA_DV), 0.5),
        'state_rwkv': nrm((DEC_BATCH, L, 2, RW_HEADS, RW_HD, RW_HD), 0.5),
        'state_ssd': nrm((DEC_BATCH, L, 2, SSD_HEADS, SSD_N, SSD_HEADDIM), 0.5),
        'cache_nat_k': nrm((DEC_BATCH, L, NAT_HEADS, PAST_LEN, NAT_HD)),
        'cache_nat_v': nrm((DEC_BATCH, L, NAT_HEADS, PAST_LEN, NAT_HD)),
        'w_mod': nrm((L, d, 6 * d), 0.5 * d ** -0.5),
        'b_mod': nrm((L, 6 * d), 0.01),
        'norm1_g': gain((L, d)),
        'norm2_g': gain((L, d)),
        'w_in': nrm((L, d, IN_COLS), d ** -0.5),
        'w_out': nrm((L, D_MIX, d), D_MIX ** -0.5),
        'gla_wa2': nrm((L, 2, GLA_RANK, GLA_HEADS * GLA_DK), GLA_RANK ** -0.5),
        'gla_ba': nrm((L, 2, GLA_HEADS * GLA_DK), 0.1),
        'gla_norm_g': gain((L, GLA_DV)),
        'rw_mu': unif((L, 2, RW_COLS), 0.0, 0.5),
        'rw_w0': -1.0 + nrm((L, 2, GROUP_W), 0.5),
        'rw_w2': nrm((L, 2, RW_W_RANK, GROUP_W), 0.5 * RW_W_RANK ** -0.5),
        'rw_a0': nrm((L, GROUP_W), 0.1),
        'rw_a2': nrm((L, RW_A_RANK, GROUP_W), RW_A_RANK ** -0.5),
        'rw_g2': nrm((L, RW_G_RANK, GROUP_W), RW_G_RANK ** -0.5),
        'rw_kk': gain((L, GROUP_W), 0.1),
        'rw_ka': gain((L, GROUP_W), 0.1),
        'rw_rk': nrm((L, RW_HEADS, RW_HD), 0.1),
        'rw_ln_g': gain((L, GROUP_W)),
        'rw_ln_b': nrm((L, GROUP_W), 0.01),
        'ssd_conv_w': nrm((L, SSD_CONV, SSD_CONV_DIM), SSD_CONV ** -0.5),
        'ssd_conv_b': nrm((L, SSD_CONV_DIM), 0.01),
        'ssd_a_log': jnp.log(unif((L, 2, SSD_HEADS), 1.0, 16.0)),
        'ssd_dt_bias': dt0 + jnp.log(-jnp.expm1(-dt0)),
        'ssd_d': gain((L, SSD_HEADS), 0.1),
        'ssd_norm_g': gain((L, GROUP_W)),
        'nat_rpb': nrm((L, NAT_HEADS, 2 * NAT_KR - 1, 2 * NAT_KC - 1), 0.1),
        'moe_wr': nrm((L, d, N_EXPERTS), d ** -0.5),
        'moe_br': nrm((L, N_EXPERTS), 0.01),
        'moe_w1': nrm((L, N_EXPERTS, d, 2 * D_FF), d ** -0.5),
        'moe_b1': nrm((L, N_EXPERTS, 2 * D_FF), 0.01),
        'moe_w2': nrm((L, N_EXPERTS, D_FF, d), D_FF ** -0.5),
        'moe_b2': nrm((L, N_EXPERTS, d), 0.01),
        'final_norm_g': gain((d,)),
    }


def reference(x_prompt, x_sample, c, c_ctx, state_gla, state_rwkv, state_ssd, cache_nat_k, cache_nat_v,
              w_mod, b_mod, norm1_g, norm2_g, w_in, w_out,
              gla_wa2, gla_ba, gla_norm_g,
              rw_mu, rw_w0, rw_w2, rw_a0, rw_a2, rw_g2, rw_kk, rw_ka, rw_rk, rw_ln_g, rw_ln_b,
              ssd_conv_w, ssd_conv_b, ssd_a_log, ssd_dt_bias, ssd_d, ssd_norm_g,
              nat_rpb,
              moe_wr, moe_br, moe_w1, moe_b1, moe_w2, moe_b2,
              final_norm_g):
    stacked = dict(w_mod=w_mod, b_mod=b_mod, norm1_g=norm1_g, norm2_g=norm2_g, w_in=w_in, w_out=w_out,
                   gla_wa2=gla_wa2, gla_ba=gla_ba, gla_norm_g=gla_norm_g,
                   rw_mu=rw_mu, rw_w0=rw_w0, rw_w2=rw_w2, rw_a0=rw_a0, rw_a2=rw_a2, rw_g2=rw_g2,
                   rw_kk=rw_kk, rw_ka=rw_ka, rw_rk=rw_rk, rw_ln_g=rw_ln_g, rw_ln_b=rw_ln_b,
                   ssd_conv_w=ssd_conv_w, ssd_conv_b=ssd_conv_b, ssd_a_log=ssd_a_log,
                   ssd_dt_bias=ssd_dt_bias, ssd_d=ssd_d, ssd_norm_g=ssd_norm_g,
                   nat_rpb=nat_rpb,
                   moe_wr=moe_wr, moe_br=moe_br, moe_w1=moe_w1, moe_b1=moe_b1, moe_w2=moe_w2, moe_b2=moe_b2)

    xc = x_prompt
    gla_l, rw_l, ssd_l, k_l, v_l = [], [], [], [], []
    for l in range(DEPTH):
        lp = {name: arr[l] for name, arr in stacked.items()}
        xc, (s_gla, s_rw, s_ssd, k_ctx, v_ctx) = trunk_layer(xc, c_ctx[None, :], lp, None)
        gla_l.append(s_gla)
        rw_l.append(s_rw)
        ssd_l.append(s_ssd)
        k_l.append(k_ctx)
        v_l.append(v_ctx)
    y_prompt = rmsnorm(xc, final_norm_g)

    xs = x_sample
    for l in range(DEPTH):
        lp = {name: arr[l] for name, arr in stacked.items()}
        ctx = (state_gla[:, l], state_rwkv[:, l], state_ssd[:, l], cache_nat_k[:, l], cache_nat_v[:, l])
        xs, _ = trunk_layer(xs, c, lp, ctx)
    y_sample = rmsnorm(xs, final_norm_g)

    new_state_gla = jnp.stack(gla_l, axis=1)
    new_state_rwkv = jnp.stack(rw_l, axis=1)
    new_state_ssd = jnp.stack(ssd_l, axis=1)
    new_cache_nat_k = jnp.stack(k_l, axis=1)
    new_cache_nat_v = jnp.stack(v_l, axis=1)
    return (y_prompt, y_sample, new_state_gla, new_state_rwkv, new_state_ssd, new_cache_nat_k, new_cache_nat_v)
```

```python
import functools

import jax
import jax.numpy as jnp
from jax import lax
from jax.experimental import pallas as pl
from jax.experimental.pallas import tpu as pltpu

D_MODEL = 1024
BATCH = 32
SEQ = 256
DEPTH = 2
DEC_BATCH = 4
DEC_SEQ = 2048
PAST_LEN = 512
GRID_W = 64
GROUP_W = 256
EPS = 1e-6

GLA_HEADS = 4
GLA_DK = 32
GLA_DV = 64
GLA_RANK = 16
GLA_GATE_NORM = 16.0
GLA_CHUNK = 64

RW_HEADS = 4
RW_HD = 64
RW_W_RANK = 32
RW_A_RANK = 32
RW_G_RANK = 64
RW_LN_EPS = 64e-5

SSD_HEADS = 4
SSD_HEADDIM = 64
SSD_GROUPS = 2
SSD_N = 64
SSD_CONV = 3
SSD_CHUNK = 64
SSD_CONV_DIM = GROUP_W + 2 * SSD_GROUPS * SSD_N

NAT_HEADS = 4
NAT_HD = 64
NAT_KR = 8
NAT_KC = 16
Q_BLOCK = 128

N_EXPERTS = 32
TOP_K = 4
D_FF = D_MODEL
SWIGLU_LIMIT = 7.0
SWIGLU_ALPHA = 1.702

GLA_SIZES = (GLA_HEADS * GLA_DK, GLA_HEADS * GLA_DK, GROUP_W, GROUP_W, GLA_RANK)
RW_SIZES = (GROUP_W, GROUP_W, GROUP_W, RW_W_RANK, RW_A_RANK, RW_G_RANK)
SSD_SIZES = (GROUP_W, SSD_CONV_DIM, 2 * SSD_HEADS)
NAT_SIZES = (GROUP_W, GROUP_W, GROUP_W)
RW_COLS = sum(RW_SIZES)
MIXER_COLS = (sum(GLA_SIZES), RW_COLS, sum(SSD_SIZES), sum(NAT_SIZES))
IN_COLS = sum(MIXER_COLS)

N_CTX = BATCH * SEQ
N_LAT = DEC_BATCH * DEC_SEQ
N_TOK = N_CTX + N_LAT

LANE = 128
VMEM_LIMIT = 56 * 1024 * 1024
MOE_TM = 256


def _split(t, sizes):
    idx, acc = [], 0
    for s in sizes[:-1]:
        acc += s
        idx.append(acc)
    return jnp.split(t, idx, axis=-1)


def _mm_kernel(x_ref, w_ref, o_ref):
    o_ref[...] = jnp.dot(x_ref[...].astype(jnp.bfloat16), w_ref[...].astype(jnp.bfloat16),
                         preferred_element_type=jnp.float32)


def _matmul(x, w, tm=512, tn=512):
    m, k = x.shape
    n = w.shape[1]
    tn = min(tn, n)
    assert n % tn == 0 and m % tm == 0, (m, n, tm, tn)
    return pl.pallas_call(
        _mm_kernel,
        grid=(n // tn, m // tm),
        in_specs=[pl.BlockSpec((tm, k), lambda j, i: (i, 0)),
                  pl.BlockSpec((k, tn), lambda j, i: (0, j))],
        out_specs=pl.BlockSpec((tm, tn), lambda j, i: (i, j)),
        out_shape=jax.ShapeDtypeStruct((m, n), jnp.float32),
        compiler_params=pltpu.CompilerParams(vmem_limit_bytes=VMEM_LIMIT),
        name="dense_matmul",
    )(x, w)


def _moe_kernel(be_ref, nu_ref, xs_ref, w1_ref, b1_ref, w2_ref, b2_ref, o_ref, w1b, w2b):
    i = pl.program_id(0)
    e = be_ref[i]
    prev = be_ref[jnp.maximum(i - 1, 0)]

    @pl.when(jnp.logical_or(i == 0, e != prev))
    def _():
        w1b[...] = w1_ref[0].astype(jnp.bfloat16)
        w2b[...] = w2_ref[0].astype(jnp.bfloat16)

    @pl.when(i < nu_ref[0])
    def _():
        gu = jnp.dot(xs_ref[...], w1b[...], preferred_element_type=jnp.float32) + b1_ref[0]
        g = jnp.minimum(gu[:, :D_FF], SWIGLU_LIMIT)
        up = jnp.clip(gu[:, D_FF:], -SWIGLU_LIMIT, SWIGLU_LIMIT)
        act = (up + 1.0) * g * jax.nn.sigmoid(SWIGLU_ALPHA * g)
        o_ref[...] = jnp.dot(act.astype(jnp.bfloat16), w2b[...],
                             preferred_element_type=jnp.float32) + b2_ref[0]

    @pl.when(i >= nu_ref[0])
    def _():
        o_ref[...] = jnp.zeros_like(o_ref)


def _moe_experts(xs, block_e, n_used, w1, b1, w2, b2):
    n_slots, d = xs.shape
    n_blocks = n_slots // MOE_TM
    grid_spec = pltpu.PrefetchScalarGridSpec(
        num_scalar_prefetch=2,
        grid=(n_blocks,),
        in_specs=[
            pl.BlockSpec((MOE_TM, d), lambda i, be, nu: (i, 0)),
            pl.BlockSpec((1, d, 2 * D_FF), lambda i, be, nu: (be[i], 0, 0)),
            pl.BlockSpec((1, 1, 2 * D_FF), lambda i, be, nu: (be[i], 0, 0)),
            pl.BlockSpec((1, D_FF, d), lambda i, be, nu: (be[i], 0, 0)),
            pl.BlockSpec((1, 1, d), lambda i, be, nu: (be[i], 0, 0)),
        ],
        out_specs=pl.BlockSpec((MOE_TM, d), lambda i, be, nu: (i, 0)),
        scratch_shapes=[pltpu.VMEM((d, 2 * D_FF), jnp.bfloat16),
                        pltpu.VMEM((D_FF, d), jnp.bfloat16)],
    )
    return pl.pallas_call(
        _moe_kernel,
        grid_spec=grid_spec,
        out_shape=jax.ShapeDtypeStruct((n_slots, d), jnp.float32),
        compiler_params=pltpu.CompilerParams(vmem_limit_bytes=VMEM_LIMIT),
        name="moe_experts",
    )(block_e, n_used, xs, w1, b1.reshape(N_EXPERTS, 1, -1), w2, b2.reshape(N_EXPERTS, 1, -1))


def _moe_ffn(h, wr, br, w1, b1, w2, b2):
    n_tok, d = h.shape
    logits = jnp.dot(h, wr, precision=lax.Precision.HIGHEST) + br
    top_v, top_i = lax.top_k(logits, TOP_K)
    gates = jax.nn.softmax(top_v, axis=-1)
    n_asg = n_tok * TOP_K
    flat_e = top_i.reshape(-1)
    order = jnp.argsort(flat_e)
    sorted_e = flat_e[order]
    tok = order // TOP_K
    counts = jnp.bincount(flat_e, length=N_EXPERTS)
    padded = (counts + MOE_TM - 1) // MOE_TM * MOE_TM
    start = jnp.cumsum(counts) - counts
    cum_p = jnp.cumsum(padded)
    pstart = cum_p - padded
    dest = pstart[sorted_e] + jnp.arange(n_asg) - start[sorted_e]
    n_slots = n_asg + N_EXPERTS * MOE_TM
    n_blocks = n_slots // MOE_TM
    xs = jnp.zeros((n_slots, d), jnp.bfloat16).at[dest].set(h.astype(jnp.bfloat16)[tok])
    block_e = jnp.minimum(jnp.sum(cum_p[None, :] <= (jnp.arange(n_blocks) * MOE_TM)[:, None], axis=1),
                          N_EXPERTS - 1).astype(jnp.int32)
    n_used = (cum_p[-1] // MOE_TM).astype(jnp.int32).reshape(1)
    ys = _moe_experts(xs, block_e, n_used, w1, b1, w2, b2)
    slot = jnp.zeros((n_asg,), jnp.int32).at[order].set(dest.astype(jnp.int32))
    y = ys[slot].reshape(n_tok, TOP_K, d) * gates[..., None]
    return jnp.sum(y, axis=1)


def to_heads(t, nh):
    b, l, _ = t.shape
    return t.reshape(b, l, nh, -1).transpose(0, 2, 1, 3)


def from_heads(t):
    b, h, l, d = t.shape
    return t.transpose(0, 2, 1, 3).reshape(b, l, h * d)


def flip_seq(t):
    return jnp.flip(t, axis=2)


def rmsnorm(x, g):
    return x * lax.rsqrt(jnp.mean(x * x, axis=-1, keepdims=True) + EPS) * g


def chunk_gla(q, k, v, log_a, s0):
    bsz, nh, seqlen, _ = q.shape
    dv = v.shape[-1]
    n = seqlen // GLA_CHUNK
    cs = lambda t: t.reshape(bsz, nh, n, GLA_CHUNK, t.shape[-1])
    q, k, v, log_a = cs(q), cs(k), cs(v), cs(log_a)
    b = jnp.cumsum(log_a, axis=3)
    b_end = b[:, :, :, -1:, :]
    q_in = q * jnp.exp(b)
    k_in = k * jnp.exp(-b)
    k_end = k * jnp.exp(b_end - b)
    causal = jnp.tril(jnp.ones((GLA_CHUNK, GLA_CHUNK), bool))
    att = jnp.where(causal, jnp.einsum('bhncd,bhnsd->bhncs', q_in, k_in), 0.0)
    o = jnp.einsum('bhncs,bhnse->bhnce', att, v)
    ds = jnp.einsum('bhnsd,bhnse->bhnde', k_end, v)
    decay = jnp.exp(b_end[:, :, :, 0, :])

    def step(s, inp):
        d, dsn = inp
        return s * d[..., None] + dsn, s

    s_fin, s_prev = lax.scan(step, s0.astype(jnp.float32),
                             (jnp.moveaxis(decay, 2, 0), jnp.moveaxis(ds, 2, 0)))
    o = o + jnp.einsum('bhncd,nbhde->bhnce', q_in, s_prev)
    return o.reshape(bsz, nh, seqlen, dv), s_fin


def head_rmsnorm(o, g):
    return o * lax.rsqrt(jnp.mean(o * o, axis=-1, keepdims=True) + EPS) * g


def gla_mixer(u, lp, s0):
    q, k, v, g, code = _split(u, GLA_SIZES)
    q = to_heads(q, GLA_HEADS) * GLA_DK ** -0.5
    k = to_heads(k, GLA_HEADS)
    v = to_heads(v, GLA_HEADS)
    la = [to_heads(jax.nn.log_sigmoid(code @ lp['gla_wa2'][d] + lp['gla_ba'][d]) / GLA_GATE_NORM, GLA_HEADS)
          for d in range(2)]
    o_f, s_f = chunk_gla(q, k, v, la[0], s0[:, 0])
    o_b, s_b = chunk_gla(flip_seq(q), flip_seq(k), flip_seq(v), flip_seq(la[1]), s0[:, 1])
    o = head_rmsnorm(o_f + flip_seq(o_b), lp['gla_norm_g'])
    return from_heads(o) * jax.nn.silu(g), jnp.stack([s_f, s_b], axis=1)


def rwkv_scan(r, w, k, v, kk, a, s0, reverse):
    def step(s, inp):
        r_t, w_t, k_t, v_t, kk_t, a_t = inp
        sa = -jnp.einsum('bhij,bhj->bhi', s, kk_t)
        s = (s * w_t[:, :, None, :] + sa[..., None] * (kk_t * a_t)[:, :, None, :]
             + v_t[..., None] * k_t[:, :, None, :])
        return s, jnp.einsum('bhij,bhj->bhi', s, r_t)
    return lax.scan(step, s0.astype(jnp.float32), (r, w, k, v, kk, a), reverse=reverse)


def rwkv_mixer(u, lp, s0):
    bsz, seqlen, _ = u.shape
    zero = jnp.zeros_like(u[:, :1])
    prev = jnp.concatenate([zero, u[:, :-1]], axis=1)
    nxt = jnp.concatenate([u[:, 1:], zero], axis=1)
    u = u + lp['rw_mu'][0] * (prev - u) + lp['rw_mu'][1] * (nxt - u)
    r, k, v, wc, ac, gc = _split(u, RW_SIZES)
    gate = jax.nn.sigmoid(gc) @ lp['rw_g2']
    a = jax.nn.sigmoid(lp['rw_a0'] + ac @ lp['rw_a2'])
    hs = lambda t: t.reshape(bsz, seqlen, RW_HEADS, RW_HD)
    kk = hs(k * lp['rw_kk'])
    kk = kk * lax.rsqrt(jnp.sum(kk * kk, axis=-1, keepdims=True) + 1e-12)
    k = k * (1.0 + (a - 1.0) * lp['rw_ka'])
    tm = lambda t: hs(t).transpose(1, 0, 2, 3)
    decays = [jnp.exp(-jnp.exp(-jax.nn.softplus(-(lp['rw_w0'][d] + jnp.tanh(wc) @ lp['rw_w2'][d])) - 0.5))
              for d in range(2)]
    kk_t = kk.transpose(1, 0, 2, 3)
    s_f, y_f = rwkv_scan(tm(r), tm(decays[0]), tm(k), tm(v), kk_t, tm(a), s0[:, 0], False)
    s_b, y_b = rwkv_scan(tm(r), tm(decays[1]), tm(k), tm(v), kk_t, tm(a), s0[:, 1], True)
    y = (y_f + y_b).transpose(1, 0, 2, 3)
    mu = jnp.mean(y, axis=-1, keepdims=True)
    var = jnp.mean(jnp.square(y - mu), axis=-1, keepdims=True)
    y = ((y - mu) * lax.rsqrt(var + RW_LN_EPS)).reshape(bsz, seqlen, GROUP_W) * lp['rw_ln_g'] + lp['rw_ln_b']
    bonus = jnp.sum(hs(r) * hs(k) * lp['rw_rk'], axis=-1, keepdims=True) * hs(v)
    y = y + bonus.reshape(bsz, seqlen, GROUP_W)
    return y * gate, jnp.stack([s_f, s_b], axis=1)


def chunk_ssd(x, dt, a_neg, bm, cm, s0):
    bsz, nh, seqlen, hp = x.shape
    n = seqlen // SSD_CHUNK
    cs = lambda t: t.reshape((bsz, nh, n, SSD_CHUNK) + t.shape[3:])
    x, dt, bm, cm = cs(x), cs(dt), cs(bm), cs(cm)
    acum = jnp.cumsum(dt * a_neg[None, :, None, None], axis=-1)
    causal = jnp.tril(jnp.ones((SSD_CHUNK, SSD_CHUNK), bool))
    lmat = jnp.exp(jnp.where(causal, acum[..., :, None] - acum[..., None, :], -jnp.inf))
    xdt = x * dt[..., None]
    scores = jnp.einsum('bhncm,bhnsm->bhncs', cm, bm) * lmat
    y = jnp.einsum('bhncs,bhnsp->bhncp', scores, xdt)
    ds = jnp.einsum('bhnsm,bhnsp->bhnmp', bm * jnp.exp(acum[..., -1:] - acum)[..., None], xdt)
    decay = jnp.exp(acum[..., -1])

    def step(s, inp):
        d, dsn = inp
        return s * d[..., None, None] + dsn, s

    s_fin, s_prev = lax.scan(step, s0.astype(jnp.float32),
                             (jnp.moveaxis(decay, 2, 0), jnp.moveaxis(ds, 2, 0)))
    y = y + jnp.einsum('bhncm,nbhmp->bhncp', cm * jnp.exp(acum)[..., None], s_prev)
    return y.reshape(bsz, nh, seqlen, hp), s_fin


def centred_dwconv(x, w, b):
    zero = jnp.zeros_like(x[:, :1])
    prev = jnp.concatenate([zero, x[:, :-1]], axis=1)
    nxt = jnp.concatenate([x[:, 1:], zero], axis=1)
    return prev * w[0] + x * w[1] + nxt * w[2] + b


def ssd_mixer(u, lp, s0):
    bsz, seqlen, _ = u.shape
    z, xbc, dt_raw = _split(u, SSD_SIZES)
    xbc = jax.nn.silu(centred_dwconv(xbc, lp['ssd_conv_w'], lp['ssd_conv_b']))
    xs, bm, cm = _split(xbc, (GROUP_W, SSD_GROUPS * SSD_N, SSD_GROUPS * SSD_N))
    x_h = to_heads(xs, SSD_HEADS)
    rep = SSD_HEADS // SSD_GROUPS
    bm = jnp.repeat(to_heads(bm, SSD_GROUPS), rep, axis=1)
    cm = jnp.repeat(to_heads(cm, SSD_GROUPS), rep, axis=1)
    dt_raw = dt_raw.reshape(bsz, seqlen, 2, SSD_HEADS)
    dts = [jax.nn.softplus(dt_raw[:, :, d] + lp['ssd_dt_bias'][d]).transpose(0, 2, 1) for d in range(2)]
    a_neg = -jnp.exp(lp['ssd_a_log'])
    y_f, s_f = chunk_ssd(x_h, dts[0], a_neg[0], bm, cm, s0[:, 0])
    y_b, s_b = chunk_ssd(flip_seq(x_h), flip_seq(dts[1]), a_neg[1], flip_seq(bm), flip_seq(cm), s0[:, 1])
    y = y_f + flip_seq(y_b) + lp['ssd_d'][None, :, None, None] * x_h
    y = from_heads(y) * jax.nn.silu(z)
    return rmsnorm(y, lp['ssd_norm_g']), jnp.stack([s_f, s_b], axis=1)


def nat_context(u):
    q, k, v = _split(u, NAT_SIZES)
    q = to_heads(q, NAT_HEADS) * NAT_HD ** -0.5
    k = to_heads(k, NAT_HEADS)
    v = to_heads(v, NAT_HEADS)
    p = jax.nn.softmax(jnp.einsum('bhqd,bhkd->bhqk', q, k), axis=-1)
    o = jnp.einsum('bhqk,bhkd->bhqd', p, v)
    return from_heads(o), k, v


def nat_latent(u, rpb, k_ctx, v_ctx):
    bsz, seqlen, _ = u.shape
    rows = seqlen // GRID_W
    kr = min(NAT_KR, rows)
    q, k, v = _split(u, NAT_SIZES)
    grid = lambda t: to_heads(t, NAT_HEADS).reshape(bsz, NAT_HEADS, rows, GRID_W, NAT_HD)
    q, k, v = grid(q) * NAT_HD ** -0.5, grid(k), grid(v)
    r = jnp.arange(rows)
    col = jnp.arange(GRID_W)
    row_idx = jnp.clip(r - kr // 2, 0, rows - kr)[:, None] + jnp.arange(kr)[None, :]
    c0 = jnp.clip(col - NAT_KC // 2, 0, GRID_W - NAT_KC)
    col_ok = (col[None, :] >= c0[:, None]) & (col[None, :] < c0[:, None] + NAT_KC)
    k_band = k[:, :, row_idx]
    v_band = v[:, :, row_idx]
    s_loc = jnp.einsum('bhrqd,bhrjkd->bhrqjk', q, k_band)
    rel_r = row_idx - r[:, None] + NAT_KR - 1
    rel_c = jnp.clip(col[None, :] - col[:, None] + NAT_KC - 1, 0, 2 * NAT_KC - 2)
    bias = rpb[:, rel_r][:, :, :, rel_c].transpose(0, 1, 3, 2, 4)
    s_loc = jnp.where(col_ok[:, None, :], s_loc + bias, -jnp.inf)
    s_ctx = jnp.einsum('bhrqd,bhcd->bhrqc', q, k_ctx)
    n_loc = kr * GRID_W
    s = jnp.concatenate([s_loc.reshape(bsz, NAT_HEADS, rows, GRID_W, n_loc), s_ctx], axis=-1)
    p = jax.nn.softmax(s.astype(jnp.float32), axis=-1)
    p_loc = p[..., :n_loc].reshape(s_loc.shape)
    p_ctx = p[..., n_loc:]
    o = (jnp.einsum('bhrqjk,bhrjkd->bhrqd', p_loc, v_band)
         + jnp.einsum('bhrqc,bhcd->bhrqd', p_ctx, v_ctx))
    return from_heads(o.reshape(bsz, NAT_HEADS, seqlen, NAT_HD))


def kernel(x_prompt, x_sample, c, c_ctx, state_gla, state_rwkv, state_ssd, cache_nat_k, cache_nat_v, w_mod, b_mod, norm1_g, norm2_g, w_in, w_out, gla_wa2, gla_ba, gla_norm_g, rw_mu, rw_w0, rw_w2, rw_a0, rw_a2, rw_g2, rw_kk, rw_ka, rw_rk, rw_ln_g, rw_ln_b, ssd_conv_w, ssd_conv_b, ssd_a_log, ssd_dt_bias, ssd_d, ssd_norm_g, nat_rpb, moe_wr, moe_br, moe_w1, moe_b1, moe_w2, moe_b2, final_norm_g):
    stacked = dict(w_mod=w_mod, b_mod=b_mod, norm1_g=norm1_g, norm2_g=norm2_g, w_in=w_in, w_out=w_out,
                   gla_wa2=gla_wa2, gla_ba=gla_ba, gla_norm_g=gla_norm_g,
                   rw_mu=rw_mu, rw_w0=rw_w0, rw_w2=rw_w2, rw_a0=rw_a0, rw_a2=rw_a2, rw_g2=rw_g2,
                   rw_kk=rw_kk, rw_ka=rw_ka, rw_rk=rw_rk, rw_ln_g=rw_ln_g, rw_ln_b=rw_ln_b,
                   ssd_conv_w=ssd_conv_w, ssd_conv_b=ssd_conv_b, ssd_a_log=ssd_a_log,
                   ssd_dt_bias=ssd_dt_bias, ssd_d=ssd_d, ssd_norm_g=ssd_norm_g,
                   nat_rpb=nat_rpb,
                   moe_wr=moe_wr, moe_br=moe_br, moe_w1=moe_w1, moe_b1=moe_b1, moe_w2=moe_w2, moe_b2=moe_b2)
    in_pad = (-IN_COLS) % LANE
    x = jnp.concatenate([x_prompt.reshape(N_CTX, D_MODEL), x_sample.reshape(N_LAT, D_MODEL)], axis=0)
    cond = jnp.concatenate([c_ctx[None, :], c], axis=0)
    tok_cond = jnp.concatenate([jnp.zeros((N_CTX,), jnp.int32),
                                1 + jnp.arange(N_LAT, dtype=jnp.int32) // DEC_SEQ])
    gla_l, rw_l, ssd_l, k_l, v_l = [], [], [], [], []
    for l in range(DEPTH):
        lp = {name: arr[l] for name, arr in stacked.items()}
        mod = jnp.dot(jax.nn.silu(cond), lp['w_mod'], precision=lax.Precision.HIGHEST) + lp['b_mod']
        sh1, sc1, g1, sh2, sc2, g2 = [m[tok_cond] for m in jnp.split(mod, 6, axis=-1)]
        h = rmsnorm(x, lp['norm1_g']) * (1.0 + sc1) + sh1
        u = _matmul(h, jnp.pad(lp['w_in'], ((0, 0), (0, in_pad))), tn=256)[:, :IN_COLS]
        ua, ub, uc, ud = _split(u, MIXER_COLS)
        ctx3 = lambda t: t[:N_CTX].reshape(BATCH, SEQ, -1)
        lat3 = lambda t: t[N_CTX:].reshape(DEC_BATCH, DEC_SEQ, -1)
        z_gla = jnp.zeros((BATCH, 2, GLA_HEADS, GLA_DK, GLA_DV), jnp.float32)
        z_rw = jnp.zeros((BATCH, 2, RW_HEADS, RW_HD, RW_HD), jnp.float32)
        z_ssd = jnp.zeros((BATCH, 2, SSD_HEADS, SSD_N, SSD_HEADDIM), jnp.float32)
        oa_c, s_gla = gla_mixer(ctx3(ua), lp, z_gla)
        ob_c, s_rw = rwkv_mixer(ctx3(ub), lp, z_rw)
        oc_c, s_ssd = ssd_mixer(ctx3(uc), lp, z_ssd)
        od_c, k_ctx, v_ctx = nat_context(ctx3(ud))
        gla_l.append(s_gla)
        rw_l.append(s_rw)
        ssd_l.append(s_ssd)
        k_l.append(k_ctx)
        v_l.append(v_ctx)
        oa_l, _ = gla_mixer(lat3(ua), lp, state_gla[:, l])
        ob_l, _ = rwkv_mixer(lat3(ub), lp, state_rwkv[:, l])
        oc_l, _ = ssd_mixer(lat3(uc), lp, state_ssd[:, l])
        od_l = nat_latent(lat3(ud), lp['nat_rpb'], cache_nat_k[:, l], cache_nat_v[:, l])
        mix_c = jnp.concatenate([oa_c, ob_c, oc_c, od_c], axis=-1).reshape(N_CTX, D_MODEL)
        mix_l = jnp.concatenate([oa_l, ob_l, oc_l, od_l], axis=-1).reshape(N_LAT, D_MODEL)
        mix = _matmul(jnp.concatenate([mix_c, mix_l], axis=0), lp['w_out'])
        x = x + g1 * mix
        h2 = rmsnorm(x, lp['norm2_g']) * (1.0 + sc2) + sh2
        ff = _moe_ffn(h2, lp['moe_wr'], lp['moe_br'], lp['moe_w1'], lp['moe_b1'], lp['moe_w2'], lp['moe_b2'])
        x = x + g2 * ff
    y = rmsnorm(x, final_norm_g)
    y_prompt = y[:N_CTX].reshape(BATCH, SEQ, D_MODEL)
    y_sample = y[N_CTX:].reshape(DEC_BATCH, DEC_SEQ, D_MODEL)
    return (y_prompt, y_sample, jnp.stack(gla_l, axis=1), jnp.stack(rw_l, axis=1), jnp.stack(ssd_l, axis=1),
            jnp.stack(k_l, axis=1), jnp.stack(v_l, axis=1))
```

```python
import functools

import jax
import jax.numpy as jnp
from jax import lax
from jax.experimental import pallas as pl
from jax.experimental.pallas import tpu as pltpu

D_MODEL = 1024
BATCH = 32
SEQ = 256
DEPTH = 2
DEC_BATCH = 4
DEC_SEQ = 2048
PAST_LEN = 512
GRID_W = 64
GROUP_W = 256
EPS = 1e-6

GLA_HEADS = 4
GLA_DK = 32
GLA_DV = 64
GLA_RANK = 16
GLA_GATE_NORM = 16.0
GLA_CHUNK = 64

RW_HEADS = 4
RW_HD = 64
RW_W_RANK = 32
RW_A_RANK = 32
RW_G_RANK = 64
RW_LN_EPS = 64e-5

SSD_HEADS = 4
SSD_HEADDIM = 64
SSD_GROUPS = 2
SSD_N = 64
SSD_CONV = 3
SSD_CHUNK = 64
SSD_CONV_DIM = GROUP_W + 2 * SSD_GROUPS * SSD_N

NAT_HEADS = 4
NAT_HD = 64
NAT_KR = 8
NAT_KC = 16
Q_BLOCK = 128

N_EXPERTS = 32
TOP_K = 4
D_FF = D_MODEL
SWIGLU_LIMIT = 7.0
SWIGLU_ALPHA = 1.702

GLA_SIZES = (GLA_HEADS * GLA_DK, GLA_HEADS * GLA_DK, GROUP_W, GROUP_W, GLA_RANK)
RW_SIZES = (GROUP_W, GROUP_W, GROUP_W, RW_W_RANK, RW_A_RANK, RW_G_RANK)
SSD_SIZES = (GROUP_W, SSD_CONV_DIM, 2 * SSD_HEADS)
NAT_SIZES = (GROUP_W, GROUP_W, GROUP_W)
RW_COLS = sum(RW_SIZES)
MIXER_COLS = (sum(GLA_SIZES), RW_COLS, sum(SSD_SIZES), sum(NAT_SIZES))
IN_COLS = sum(MIXER_COLS)

N_CTX = BATCH * SEQ
N_LAT = DEC_BATCH * DEC_SEQ
N_TOK = N_CTX + N_LAT

LANE = 128
VMEM_LIMIT = 56 * 1024 * 1024
MOE_TM = 256


def _split(t, sizes):
    idx, acc = [], 0
    for s in sizes[:-1]:
        acc += s
        idx.append(acc)
    return jnp.split(t, idx, axis=-1)


def _mm_kernel(x_ref, w_ref, o_ref):
    o_ref[...] = jnp.dot(x_ref[...].astype(jnp.bfloat16), w_ref[...].astype(jnp.bfloat16),
                         preferred_element_type=jnp.float32)


def _matmul(x, w, tm=512, tn=512):
    m, k = x.shape
    n = w.shape[1]
    tn = min(tn, n)
    assert n % tn == 0 and m % tm == 0, (m, n, tm, tn)
    return pl.pallas_call(
        _mm_kernel,
        grid=(n // tn, m // tm),
        in_specs=[pl.BlockSpec((tm, k), lambda j, i: (i, 0)),
                  pl.BlockSpec((k, tn), lambda j, i: (0, j))],
        out_specs=pl.BlockSpec((tm, tn), lambda j, i: (i, j)),
        out_shape=jax.ShapeDtypeStruct((m, n), jnp.float32),
        compiler_params=pltpu.CompilerParams(vmem_limit_bytes=VMEM_LIMIT),
        name="dense_matmul",
    )(x, w)


def _moe_kernel(be_ref, nu_ref, xs_ref, w1_ref, b1_ref, w2_ref, b2_ref, o_ref, w1b, w2b):
    i = pl.program_id(0)
    e = be_ref[i]
    prev = be_ref[jnp.maximum(i - 1, 0)]

    @pl.when(jnp.logical_or(i == 0, e != prev))
    def _():
        w1b[...] = w1_ref[0].astype(jnp.bfloat16)
        w2b[...] = w2_ref[0].astype(jnp.bfloat16)

    @pl.when(i < nu_ref[0])
    def _():
        gu = jnp.dot(xs_ref[...], w1b[...], preferred_element_type=jnp.float32) + b1_ref[0]
        g = jnp.minimum(gu[:, :D_FF], SWIGLU_LIMIT)
        up = jnp.clip(gu[:, D_FF:], -SWIGLU_LIMIT, SWIGLU_LIMIT)
        act = (up + 1.0) * g * jax.nn.sigmoid(SWIGLU_ALPHA * g)
        o_ref[...] = jnp.dot(act.astype(jnp.bfloat16), w2b[...],
                             preferred_element_type=jnp.float32) + b2_ref[0]

    @pl.when(i >= nu_ref[0])
    def _():
        o_ref[...] = jnp.zeros_like(o_ref)


def _moe_experts(xs, block_e, n_used, w1, b1, w2, b2):
    n_slots, d = xs.shape
    n_blocks = n_slots // MOE_TM
    grid_spec = pltpu.PrefetchScalarGridSpec(
        num_scalar_prefetch=2,
        grid=(n_blocks,),
        in_specs=[
            pl.BlockSpec((MOE_TM, d), lambda i, be, nu: (i, 0)),
            pl.BlockSpec((1, d, 2 * D_FF), lambda i, be, nu: (be[i], 0, 0)),
            pl.BlockSpec((1, 1, 2 * D_FF), lambda i, be, nu: (be[i], 0, 0)),
            pl.BlockSpec((1, D_FF, d), lambda i, be, nu: (be[i], 0, 0)),
            pl.BlockSpec((1, 1, d), lambda i, be, nu: (be[i], 0, 0)),
        ],
        out_specs=pl.BlockSpec((MOE_TM, d), lambda i, be, nu: (i, 0)),
        scratch_shapes=[pltpu.VMEM((d, 2 * D_FF), jnp.bfloat16),
                        pltpu.VMEM((D_FF, d), jnp.bfloat16)],
    )
    return pl.pallas_call(
        _moe_kernel,
        grid_spec=grid_spec,
        out_shape=jax.ShapeDtypeStruct((n_slots, d), jnp.float32),
        compiler_params=pltpu.CompilerParams(vmem_limit_bytes=VMEM_LIMIT),
        name="moe_experts",
    )(block_e, n_used, xs, w1, b1.reshape(N_EXPERTS, 1, -1), w2, b2.reshape(N_EXPERTS, 1, -1))


def _moe_ffn(h, wr, br, w1, b1, w2, b2):
    n_tok, d = h.shape
    logits = jnp.dot(h, wr, precision=lax.Precision.HIGHEST) + br
    top_v, top_i = lax.top_k(logits, TOP_K)
    gates = jax.nn.softmax(top_v, axis=-1)
    n_asg = n_tok * TOP_K
    flat_e = top_i.reshape(-1)
    order = jnp.argsort(flat_e)
    sorted_e = flat_e[order]
    tok = order // TOP_K
    counts = jnp.bincount(flat_e, length=N_EXPERTS)
    padded = (counts + MOE_TM - 1) // MOE_TM * MOE_TM
    start = jnp.cumsum(counts) - counts
    cum_p = jnp.cumsum(padded)
    pstart = cum_p - padded
    dest = pstart[sorted_e] + jnp.arange(n_asg) - start[sorted_e]
    n_slots = n_asg + N_EXPERTS * MOE_TM
    n_blocks = n_slots // MOE_TM
    tok_of_slot = jnp.zeros((n_slots,), jnp.int32).at[dest].set(tok.astype(jnp.int32))
    xs = h.astype(jnp.bfloat16)[tok_of_slot]
    block_e = jnp.minimum(jnp.sum(cum_p[None, :] <= (jnp.arange(n_blocks) * MOE_TM)[:, None], axis=1),
                          N_EXPERTS - 1).astype(jnp.int32)
    n_used = (cum_p[-1] // MOE_TM).astype(jnp.int32).reshape(1)
    ys = _moe_experts(xs, block_e, n_used, w1, b1, w2, b2)
    slot = jnp.zeros((n_asg,), jnp.int32).at[order].set(dest.astype(jnp.int32))
    y = ys[slot].reshape(n_tok, TOP_K, d) * gates[..., None]
    return jnp.sum(y, axis=1)


def to_heads(t, nh):
    b, l, _ = t.shape
    return t.reshape(b, l, nh, -1).transpose(0, 2, 1, 3)


def from_heads(t):
    b, h, l, d = t.shape
    return t.transpose(0, 2, 1, 3).reshape(b, l, h * d)


def flip_seq(t):
    return jnp.flip(t, axis=2)


def rmsnorm(x, g):
    return x * lax.rsqrt(jnp.mean(x * x, axis=-1, keepdims=True) + EPS) * g


def chunk_gla(q, k, v, log_a, s0):
    bsz, nh, seqlen, _ = q.shape
    dv = v.shape[-1]
    n = seqlen // GLA_CHUNK
    cs = lambda t: t.reshape(bsz, nh, n, GLA_CHUNK, t.shape[-1])
    q, k, v, log_a = cs(q), cs(k), cs(v), cs(log_a)
    b = jnp.cumsum(log_a, axis=3)
    b_end = b[:, :, :, -1:, :]
    q_in = q * jnp.exp(b)
    k_in = k * jnp.exp(-b)
    k_end = k * jnp.exp(b_end - b)
    causal = jnp.tril(jnp.ones((GLA_CHUNK, GLA_CHUNK), bool))
    att = jnp.where(causal, jnp.einsum('bhncd,bhnsd->bhncs', q_in, k_in), 0.0)
    o = jnp.einsum('bhncs,bhnse->bhnce', att, v)
    ds = jnp.einsum('bhnsd,bhnse->bhnde', k_end, v)
    decay = jnp.exp(b_end[:, :, :, 0, :])

    def step(s, inp):
        d, dsn = inp
        return s * d[..., None] + dsn, s

    s_fin, s_prev = lax.scan(step, s0.astype(jnp.float32),
                             (jnp.moveaxis(decay, 2, 0), jnp.moveaxis(ds, 2, 0)))
    o = o + jnp.einsum('bhncd,nbhde->bhnce', q_in, s_prev)
    return o.reshape(bsz, nh, seqlen, dv), s_fin


def head_rmsnorm(o, g):
    return o * lax.rsqrt(jnp.mean(o * o, axis=-1, keepdims=True) + EPS) * g


def gla_mixer(u, lp, s0):
    q, k, v, g, code = _split(u, GLA_SIZES)
    q = to_heads(q, GLA_HEADS) * GLA_DK ** -0.5
    k = to_heads(k, GLA_HEADS)
    v = to_heads(v, GLA_HEADS)
    la = [to_heads(jax.nn.log_sigmoid(code @ lp['gla_wa2'][d] + lp['gla_ba'][d]) / GLA_GATE_NORM, GLA_HEADS)
          for d in range(2)]
    o_f, s_f = chunk_gla(q, k, v, la[0], s0[:, 0])
    o_b, s_b = chunk_gla(flip_seq(q), flip_seq(k), flip_seq(v), flip_seq(la[1]), s0[:, 1])
    o = head_rmsnorm(o_f + flip_seq(o_b), lp['gla_norm_g'])
    return from_heads(o) * jax.nn.silu(g), jnp.stack([s_f, s_b], axis=1)


SUBLANE = 8
RW_TB = 32


def _rwkv_kernel(w_ref, b_ref, k_ref, r_ref, kkn_ref, v_ref, kk0_ref, s0_ref, y_ref, sout_ref, s_scr, sa_scr,
                 *, niv, tb):
    t_blk = pl.program_id(1)
    row = lambda ref, s, j: jnp.broadcast_to(ref[0, s, pl.ds(j, 1), :], (SUBLANE, LANE))

    @pl.when(t_blk == 0)
    def _():
        s_scr[...] = s0_ref[0]
        for a in range(niv):
            acc = jnp.zeros((SUBLANE, LANE), jnp.float32)
            for j in range(RW_HD):
                acc = acc + s0_ref[0, j, pl.ds(a * SUBLANE, SUBLANE), :] * jnp.broadcast_to(
                    kk0_ref[0, pl.ds(j, 1), :], (SUBLANE, LANE))
            sa_scr[pl.ds(a * SUBLANE, SUBLANE), :] = -acc

    def step(s, carry):
        for a in range(niv):
            rows = pl.ds(a * SUBLANE, SUBLANE)
            sa = sa_scr[rows, :]
            v = v_ref[0, s, rows, :]
            acc_y = [jnp.zeros((SUBLANE, LANE), jnp.float32) for _ in range(2)]
            acc_s = [jnp.zeros((SUBLANE, LANE), jnp.float32) for _ in range(2)]
            for j in range(RW_HD):
                st = (s_scr[j, rows, :] * row(w_ref, s, j) + sa * row(b_ref, s, j)) + v * row(k_ref, s, j)
                s_scr[j, rows, :] = st
                acc_y[j % 2] = acc_y[j % 2] + st * row(r_ref, s, j)
                acc_s[j % 2] = acc_s[j % 2] + st * row(kkn_ref, s, j)
            y_ref[0, s, rows, :] = acc_y[0] + acc_y[1]
            sa_scr[rows, :] = -(acc_s[0] + acc_s[1])
        return carry

    lax.fori_loop(0, tb, step, 0)

    @pl.when(t_blk == pl.num_programs(1) - 1)
    def _():
        sout_ref[0] = s_scr[...]


def _rwkv_recurrence(w, b, k, r, kkn, v, kk0, s0):
    g, seqlen, _, _ = w.shape
    ni = v.shape[2]
    tb = RW_TB
    key_spec = pl.BlockSpec((1, tb, RW_HD, LANE), lambda gi, ti: (gi, ti, 0, 0))
    val_spec = pl.BlockSpec((1, tb, ni, LANE), lambda gi, ti: (gi, ti, 0, 0))
    st_spec = pl.BlockSpec((1, RW_HD, ni, LANE), lambda gi, ti: (gi, 0, 0, 0))
    return pl.pallas_call(
        functools.partial(_rwkv_kernel, niv=ni // SUBLANE, tb=tb),
        grid=(g, seqlen // tb),
        in_specs=[key_spec] * 5 + [val_spec, pl.BlockSpec((1, RW_HD, LANE), lambda gi, ti: (gi, 0, 0)), st_spec],
        out_specs=[val_spec, st_spec],
        out_shape=[jax.ShapeDtypeStruct((g, seqlen, ni, LANE), jnp.float32),
                   jax.ShapeDtypeStruct((g, RW_HD, ni, LANE), jnp.float32)],
        scratch_shapes=[pltpu.VMEM((RW_HD, ni, LANE), jnp.float32), pltpu.VMEM((ni, LANE), jnp.float32)],
        compiler_params=pltpu.CompilerParams(vmem_limit_bytes=VMEM_LIMIT),
        name="rwkv7_recurrence",
    )(w, b, k, r, kkn, v, kk0, s0)


def rwkv_scan2(r, w2, k, v, kk, b, s0):
    bsz, seqlen, _ = r.shape
    nchain = 2 * bsz * RW_HEADS
    rep = max(LANE // nchain, 1)
    ch = LANE // rep
    g = nchain // ch
    ni = RW_HD // rep
    both = lambda t: jnp.stack([t, jnp.flip(t, axis=1)], axis=0)

    def key_layout(t):
        t = t.reshape(2, bsz, seqlen, RW_HEADS, RW_HD).transpose(2, 4, 0, 1, 3)
        t = t.reshape(seqlen, RW_HD, g, ch).transpose(2, 0, 1, 3)
        return jnp.tile(t, (1, 1, 1, rep))

    kkd = both(kk)
    kkn = jnp.concatenate([kkd[:, :, 1:], jnp.zeros_like(kkd[:, :, :1])], axis=2)
    w_flipped = jnp.stack([w2[0], jnp.flip(w2[1], axis=1)], axis=0)
    vd = both(v).reshape(2, bsz, seqlen, RW_HEADS, rep, ni).transpose(2, 5, 4, 0, 1, 3)
    vd = vd.reshape(seqlen, ni, rep, g, ch).transpose(3, 0, 1, 2, 4).reshape(g, seqlen, ni, LANE)
    s0l = s0.transpose(1, 0, 2, 3, 4).reshape(2, bsz, RW_HEADS, rep, ni, RW_HD).transpose(5, 4, 3, 0, 1, 2)
    s0l = s0l.reshape(RW_HD, ni, rep, g, ch).transpose(3, 0, 1, 2, 4).reshape(g, RW_HD, ni, LANE)
    kk0 = key_layout(kkd)[:, 0]
    y, s_fin = _rwkv_recurrence(key_layout(w_flipped), key_layout(both(b)), key_layout(both(k)),
                                key_layout(both(r)), key_layout(kkn), vd, kk0, s0l)
    y = y.reshape(g, seqlen, ni, rep, ch).transpose(1, 2, 3, 0, 4).reshape(seqlen, ni, rep, 2, bsz, RW_HEADS)
    y = y.transpose(3, 4, 0, 5, 2, 1).reshape(2, bsz, seqlen, GROUP_W)
    s_fin = s_fin.reshape(g, RW_HD, ni, rep, ch).transpose(1, 2, 3, 0, 4)
    s_fin = s_fin.reshape(RW_HD, ni, rep, 2, bsz, RW_HEADS).transpose(4, 3, 5, 2, 1, 0)
    return y[0] + jnp.flip(y[1], axis=1), s_fin.reshape(bsz, 2, RW_HEADS, RW_HD, RW_HD)


def rwkv_mixer(u, lp, s0):
    bsz, seqlen, _ = u.shape
    zero = jnp.zeros_like(u[:, :1])
    prev = jnp.concatenate([zero, u[:, :-1]], axis=1)
    nxt = jnp.concatenate([u[:, 1:], zero], axis=1)
    u = u + lp['rw_mu'][0] * (prev - u) + lp['rw_mu'][1] * (nxt - u)
    r, k, v, wc, ac, gc = _split(u, RW_SIZES)
    gate = jax.nn.sigmoid(gc) @ lp['rw_g2']
    a = jax.nn.sigmoid(lp['rw_a0'] + ac @ lp['rw_a2'])
    hs = lambda t: t.reshape(bsz, seqlen, RW_HEADS, RW_HD)
    kk = hs(k * lp['rw_kk'])
    kk = (kk * lax.rsqrt(jnp.sum(kk * kk, axis=-1, keepdims=True) + 1e-12)).reshape(bsz, seqlen, GROUP_W)
    k = k * (1.0 + (a - 1.0) * lp['rw_ka'])
    decays = jnp.stack([jnp.exp(-jnp.exp(-jax.nn.softplus(-(lp['rw_w0'][d] + jnp.tanh(wc) @ lp['rw_w2'][d])) - 0.5))
                        for d in range(2)], axis=0)
    y, s_new = rwkv_scan2(r, decays, k, v, kk, kk * a, s0)
    y = hs(y)
    mu = jnp.mean(y, axis=-1, keepdims=True)
    var = jnp.mean(jnp.square(y - mu), axis=-1, keepdims=True)
    y = ((y - mu) * lax.rsqrt(var + RW_LN_EPS)).reshape(bsz, seqlen, GROUP_W) * lp['rw_ln_g'] + lp['rw_ln_b']
    bonus = jnp.sum(hs(r) * hs(k) * lp['rw_rk'], axis=-1, keepdims=True) * hs(v)
    y = y + bonus.reshape(bsz, seqlen, GROUP_W)
    return y * gate, s_new


def chunk_ssd(x, dt, a_neg, bm, cm, s0):
    bsz, nh, seqlen, hp = x.shape
    n = seqlen // SSD_CHUNK
    cs = lambda t: t.reshape((bsz, nh, n, SSD_CHUNK) + t.shape[3:])
    x, dt, bm, cm = cs(x), cs(dt), cs(bm), cs(cm)
    acum = jnp.cumsum(dt * a_neg[None, :, None, None], axis=-1)
    causal = jnp.tril(jnp.ones((SSD_CHUNK, SSD_CHUNK), bool))
    lmat = jnp.exp(jnp.where(causal, acum[..., :, None] - acum[..., None, :], -jnp.inf))
    xdt = x * dt[..., None]
    scores = jnp.einsum('bhncm,bhnsm->bhncs', cm, bm) * lmat
    y = jnp.einsum('bhncs,bhnsp->bhncp', scores, xdt)
    ds = jnp.einsum('bhnsm,bhnsp->bhnmp', bm * jnp.exp(acum[..., -1:] - acum)[..., None], xdt)
    decay = jnp.exp(acum[..., -1])

    def step(s, inp):
        d, dsn = inp
        return s * d[..., None, None] + dsn, s

    s_fin, s_prev = lax.scan(step, s0.astype(jnp.float32),
                             (jnp.moveaxis(decay, 2, 0), jnp.moveaxis(ds, 2, 0)))
    y = y + jnp.einsum('bhncm,nbhmp->bhncp', cm * jnp.exp(acum)[..., None], s_prev)
    return y.reshape(bsz, nh, seqlen, hp), s_fin


def centred_dwconv(x, w, b):
    zero = jnp.zeros_like(x[:, :1])
    prev = jnp.concatenate([zero, x[:, :-1]], axis=1)
    nxt = jnp.concatenate([x[:, 1:], zero], axis=1)
    return prev * w[0] + x * w[1] + nxt * w[2] + b


def ssd_mixer(u, lp, s0):
    bsz, seqlen, _ = u.shape
    z, xbc, dt_raw = _split(u, SSD_SIZES)
    xbc = jax.nn.silu(centred_dwconv(xbc, lp['ssd_conv_w'], lp['ssd_conv_b']))
    xs, bm, cm = _split(xbc, (GROUP_W, SSD_GROUPS * SSD_N, SSD_GROUPS * SSD_N))
    x_h = to_heads(xs, SSD_HEADS)
    rep = SSD_HEADS // SSD_GROUPS
    bm = jnp.repeat(to_heads(bm, SSD_GROUPS), rep, axis=1)
    cm = jnp.repeat(to_heads(cm, SSD_GROUPS), rep, axis=1)
    dt_raw = dt_raw.reshape(bsz, seqlen, 2, SSD_HEADS)
    dts = [jax.nn.softplus(dt_raw[:, :, d] + lp['ssd_dt_bias'][d]).transpose(0, 2, 1) for d in range(2)]
    a_neg = -jnp.exp(lp['ssd_a_log'])
    y_f, s_f = chunk_ssd(x_h, dts[0], a_neg[0], bm, cm, s0[:, 0])
    y_b, s_b = chunk_ssd(flip_seq(x_h), flip_seq(dts[1]), a_neg[1], flip_seq(bm), flip_seq(cm), s0[:, 1])
    y = y_f + flip_seq(y_b) + lp['ssd_d'][None, :, None, None] * x_h
    y = from_heads(y) * jax.nn.silu(z)
    return rmsnorm(y, lp['ssd_norm_g']), jnp.stack([s_f, s_b], axis=1)


def nat_context(u):
    q, k, v = _split(u, NAT_SIZES)
    q = to_heads(q, NAT_HEADS) * NAT_HD ** -0.5
    k = to_heads(k, NAT_HEADS)
    v = to_heads(v, NAT_HEADS)
    p = jax.nn.softmax(jnp.einsum('bhqd,bhkd->bhqk', q, k), axis=-1)
    o = jnp.einsum('bhqk,bhkd->bhqd', p, v)
    return from_heads(o), k, v


def nat_latent(u, rpb, k_ctx, v_ctx):
    bsz, seqlen, _ = u.shape
    rows = seqlen // GRID_W
    kr = min(NAT_KR, rows)
    q, k, v = _split(u, NAT_SIZES)
    grid = lambda t: to_heads(t, NAT_HEADS).reshape(bsz, NAT_HEADS, rows, GRID_W, NAT_HD)
    q, k, v = grid(q) * NAT_HD ** -0.5, grid(k), grid(v)
    r = jnp.arange(rows)
    col = jnp.arange(GRID_W)
    row_idx = jnp.clip(r - kr // 2, 0, rows - kr)[:, None] + jnp.arange(kr)[None, :]
    c0 = jnp.clip(col - NAT_KC // 2, 0, GRID_W - NAT_KC)
    col_ok = (col[None, :] >= c0[:, None]) & (col[None, :] < c0[:, None] + NAT_KC)
    k_band = k[:, :, row_idx]
    v_band = v[:, :, row_idx]
    s_loc = jnp.einsum('bhrqd,bhrjkd->bhrqjk', q, k_band)
    rel_r = row_idx - r[:, None] + NAT_KR - 1
    rel_c = jnp.clip(col[None, :] - col[:, None] + NAT_KC - 1, 0, 2 * NAT_KC - 2)
    bias = rpb[:, rel_r][:, :, :, rel_c].transpose(0, 1, 3, 2, 4)
    s_loc = jnp.where(col_ok[:, None, :], s_loc + bias, -jnp.inf)
    s_ctx = jnp.einsum('bhrqd,bhcd->bhrqc', q, k_ctx)
    n_loc = kr * GRID_W
    s = jnp.concatenate([s_loc.reshape(bsz, NAT_HEADS, rows, GRID_W, n_loc), s_ctx], axis=-1)
    p = jax.nn.softmax(s.astype(jnp.float32), axis=-1)
    p_loc = p[..., :n_loc].reshape(s_loc.shape)
    p_ctx = p[..., n_loc:]
    o = (jnp.einsum('bhrqjk,bhrjkd->bhrqd', p_loc, v_band)
         + jnp.einsum('bhrqc,bhcd->bhrqd', p_ctx, v_ctx))
    return from_heads(o.reshape(bsz, NAT_HEADS, seqlen, NAT_HD))


def kernel(x_prompt, x_sample, c, c_ctx, state_gla, state_rwkv, state_ssd, cache_nat_k, cache_nat_v, w_mod, b_mod, norm1_g, norm2_g, w_in, w_out, gla_wa2, gla_ba, gla_norm_g, rw_mu, rw_w0, rw_w2, rw_a0, rw_a2, rw_g2, rw_kk, rw_ka, rw_rk, rw_ln_g, rw_ln_b, ssd_conv_w, ssd_conv_b, ssd_a_log, ssd_dt_bias, ssd_d, ssd_norm_g, nat_rpb, moe_wr, moe_br, moe_w1, moe_b1, moe_w2, moe_b2, final_norm_g):
    stacked = dict(w_mod=w_mod, b_mod=b_mod, norm1_g=norm1_g, norm2_g=norm2_g, w_in=w_in, w_out=w_out,
                   gla_wa2=gla_wa2, gla_ba=gla_ba, gla_norm_g=gla_norm_g,
                   rw_mu=rw_mu, rw_w0=rw_w0, rw_w2=rw_w2, rw_a0=rw_a0, rw_a2=rw_a2, rw_g2=rw_g2,
                   rw_kk=rw_kk, rw_ka=rw_ka, rw_rk=rw_rk, rw_ln_g=rw_ln_g, rw_ln_b=rw_ln_b,
                   ssd_conv_w=ssd_conv_w, ssd_conv_b=ssd_conv_b, ssd_a_log=ssd_a_log,
                   ssd_dt_bias=ssd_dt_bias, ssd_d=ssd_d, ssd_norm_g=ssd_norm_g,
                   nat_rpb=nat_rpb,
                   moe_wr=moe_wr, moe_br=moe_br, moe_w1=moe_w1, moe_b1=moe_b1, moe_w2=moe_w2, moe_b2=moe_b2)
    in_pad = (-IN_COLS) % LANE
    x = jnp.concatenate([x_prompt.reshape(N_CTX, D_MODEL), x_sample.reshape(N_LAT, D_MODEL)], axis=0)
    cond = jnp.concatenate([c_ctx[None, :], c], axis=0)
    tok_cond = jnp.concatenate([jnp.zeros((N_CTX,), jnp.int32),
                                1 + jnp.arange(N_LAT, dtype=jnp.int32) // DEC_SEQ])
    gla_l, rw_l, ssd_l, k_l, v_l = [], [], [], [], []
    for l in range(DEPTH):
        lp = {name: arr[l] for name, arr in stacked.items()}
        mod = jnp.dot(jax.nn.silu(cond), lp['w_mod'], precision=lax.Precision.HIGHEST) + lp['b_mod']
        sh1, sc1, g1, sh2, sc2, g2 = [m[tok_cond] for m in jnp.split(mod, 6, axis=-1)]
        h = rmsnorm(x, lp['norm1_g']) * (1.0 + sc1) + sh1
        u = _matmul(h, jnp.pad(lp['w_in'], ((0, 0), (0, in_pad))), tn=256)[:, :IN_COLS]
        ua, ub, uc, ud = _split(u, MIXER_COLS)
        ctx3 = lambda t: t[:N_CTX].reshape(BATCH, SEQ, -1)
        lat3 = lambda t: t[N_CTX:].reshape(DEC_BATCH, DEC_SEQ, -1)
        z_gla = jnp.zeros((BATCH, 2, GLA_HEADS, GLA_DK, GLA_DV), jnp.float32)
        z_rw = jnp.zeros((BATCH, 2, RW_HEADS, RW_HD, RW_HD), jnp.float32)
        z_ssd = jnp.zeros((BATCH, 2, SSD_HEADS, SSD_N, SSD_HEADDIM), jnp.float32)
        oa_c, s_gla = gla_mixer(ctx3(ua), lp, z_gla)
        ob_c, s_rw = rwkv_mixer(ctx3(ub), lp, z_rw)
        oc_c, s_ssd = ssd_mixer(ctx3(uc), lp, z_ssd)
        od_c, k_ctx, v_ctx = nat_context(ctx3(ud))
        gla_l.append(s_gla)
        rw_l.append(s_rw)
        ssd_l.append(s_ssd)
        k_l.append(k_ctx)
        v_l.append(v_ctx)
        oa_l, _ = gla_mixer(lat3(ua), lp, state_gla[:, l])
        ob_l, _ = rwkv_mixer(lat3(ub), lp, state_rwkv[:, l])
        oc_l, _ = ssd_mixer(lat3(uc), lp, state_ssd[:, l])
        od_l = nat_latent(lat3(ud), lp['nat_rpb'], cache_nat_k[:, l], cache_nat_v[:, l])
        mix_c = jnp.concatenate([oa_c, ob_c, oc_c, od_c], axis=-1).reshape(N_CTX, D_MODEL)
        mix_l = jnp.concatenate([oa_l, ob_l, oc_l, od_l], axis=-1).reshape(N_LAT, D_MODEL)
        mix = _matmul(jnp.concatenate([mix_c, mix_l], axis=0), lp['w_out'])
        x = x + g1 * mix
        h2 = rmsnorm(x, lp['norm2_g']) * (1.0 + sc2) + sh2
        ff = _moe_ffn(h2, lp['moe_wr'], lp['moe_br'], lp['moe_w1'], lp['moe_b1'], lp['moe_w2'], lp['moe_b2'])
        x = x + g2 * ff
    y = rmsnorm(x, final_norm_g)
    y_prompt = y[:N_CTX].reshape(BATCH, SEQ, D_MODEL)
    y_sample = y[N_CTX:].reshape(DEC_BATCH, DEC_SEQ, D_MODEL)
    return (y_prompt, y_sample, jnp.stack(gla_l, axis=1), jnp.stack(rw_l, axis=1), jnp.stack(ssd_l, axis=1),
            jnp.stack(k_l, axis=1), jnp.stack(v_l, axis=1))
```
